```python
import math
import jax
import jax.numpy as jnp
from jax import lax
import numpy as np

D_MODEL = 2048
BATCH = 1
SEQ = 16384
DEPTH = 1
DEC_BATCH = 32
DEC_SEQ = 32
PAST_LEN = 1024

CHUNK = 64
MIX_W = D_MODEL
LRU_W = MIX_W // 2
LRU_HEADS = 8
LRU_BLK = LRU_W // LRU_HEADS
CONV_W = 4
LRU_C = 8.0
FOX_HEADS = 8
FOX_HEAD_DIM = (MIX_W - LRU_W) // FOX_HEADS
FOX_W = FOX_HEADS * FOX_HEAD_DIM
Q_BLOCK = 128
IN_COLS = 2 * LRU_W + 3 * FOX_W + FOX_HEADS
N_MEM = 256
CA_HEADS = 4
CA_HEAD_DIM = 128
CA_W = CA_HEADS * CA_HEAD_DIM
N_GROUPS = 4
EXPERTS_PER_GROUP = 8
N_EXPERTS = N_GROUPS * EXPERTS_PER_GROUP
TOP_K = 2
D_EXPERT = D_MODEL // 2
MOE_BLOCK = 128
EPS = 1e-6

kernel_name = 'hymba_style_rglru_fox_hmoe_stream_step'


def rmsnorm(x, g):
    xf = x.astype(jnp.float32)
    y = xf * lax.rsqrt(jnp.mean(xf * xf, axis=-1, keepdims=True) + EPS)
    return (y * g.astype(jnp.float32)).astype(x.dtype)


def split_in(u):
    B, T, _ = u.shape
    hs = (B, T, FOX_HEADS, FOX_HEAD_DIM)
    o = 2 * LRU_W
    xb = u[..., :LRU_W]
    gb = u[..., LRU_W:o]
    q = u[..., o:o + FOX_W].reshape(hs)
    k = u[..., o + FOX_W:o + 2 * FOX_W].reshape(hs)
    v = u[..., o + 2 * FOX_W:o + 3 * FOX_W].reshape(hs)
    fl = u[..., o + 3 * FOX_W:]
    return xb, gb, q, k, v, fl


def causal_conv(x, prev, w, b):
    T = x.shape[1]
    xp = jnp.concatenate([prev.astype(x.dtype), x], axis=1)
    y = b + sum(xp[:, j:j + T] * w[j] for j in range(CONV_W))
    return y, xp[:, -(CONV_W - 1):]


def _lin_combine(left, right):
    a1, b1 = left
    a2, b2 = right
    return a1 * a2, a2 * b1 + b2


def rglru(xc, w_rg, b_rg, w_ig, b_ig, lam, h0):
    B, T, _ = xc.shape
    xf = xc.astype(jnp.float32)
    xh = xf.reshape(B, T, LRU_HEADS, LRU_BLK)
    r = jax.nn.sigmoid(jnp.einsum('bthi,hij->bthj', xh, w_rg.astype(jnp.float32)) + b_rg).reshape(B, T, LRU_W)
    i = jax.nn.sigmoid(jnp.einsum('bthi,hij->bthj', xh, w_ig.astype(jnp.float32)) + b_ig).reshape(B, T, LRU_W)
    log_a = -LRU_C * r * jax.nn.softplus(-lam.astype(jnp.float32))
    a = jnp.exp(log_a)
    bx = jnp.sqrt(-jnp.expm1(2.0 * log_a)) * (i * xf)
    bx = bx.at[:, 0].add(a[:, 0] * h0.astype(jnp.float32))
    _, h = lax.associative_scan(_lin_combine, (a, bx), axis=1)
    return h.astype(xc.dtype), h[:, -1].astype(xc.dtype)


def fox_attend(q, k, v, fq, fk, q_pos, k_pos):
    s = jnp.einsum('bqhd,bkhd->bhqk', q, k, preferred_element_type=jnp.float32) * (FOX_HEAD_DIM ** -0.5)
    s = s + (jnp.swapaxes(fq, 1, 2)[..., :, None] - jnp.swapaxes(fk, 1, 2)[..., None, :])
    s = jnp.where(k_pos[None, :] <= q_pos[:, None], s, -jnp.inf)
    p = jax.nn.softmax(s, axis=-1)
    return jnp.einsum('bhqk,bkhd->bqhd', p.astype(v.dtype), v)


def fox_mixer(q, k, v, logf, past_k, past_v, past_logf):
    B, T, H, Dh = q.shape
    if past_k is None:
        past = 0
        k_all, v_all, lf_all = k, v, logf
    else:
        past = past_k.shape[1]
        k_all = jnp.concatenate([past_k.astype(k.dtype), k], axis=1)
        v_all = jnp.concatenate([past_v.astype(v.dtype), v], axis=1)
        lf_all = jnp.concatenate([past_logf.astype(jnp.float32), logf], axis=1)
    F = jnp.cumsum(lf_all.astype(jnp.float32), axis=1)
    fq = F[:, past:]
    k_pos = jnp.arange(past + T)
    q_pos = past + jnp.arange(T)
    if T <= Q_BLOCK:
        return fox_attend(q, k_all, v_all, fq, F, q_pos, k_pos)
    nb = T // Q_BLOCK
    qb = jnp.moveaxis(q.reshape(B, nb, Q_BLOCK, H, Dh), 1, 0)
    fqb = jnp.moveaxis(fq.reshape(B, nb, Q_BLOCK, H), 1, 0)
    pb = q_pos.reshape(nb, Q_BLOCK)
    ob = lax.map(lambda a: fox_attend(a[0], k_all, v_all, a[1], F, a[2], k_pos), (qb, fqb, pb))
    return jnp.moveaxis(ob, 0, 1).reshape(B, T, H, Dh)


def cross_attend(h, mk, mv, w_cq, w_co):
    B, T, _ = h.shape
    q = (h @ w_cq).reshape(B, T, CA_HEADS, CA_HEAD_DIM)
    s = jnp.einsum('bqhd,bkhd->bhqk', q, mk.astype(q.dtype), preferred_element_type=jnp.float32) * (CA_HEAD_DIM ** -0.5)
    p = jax.nn.softmax(s, axis=-1)
    o = jnp.einsum('bhqk,bkhd->bqhd', p.astype(h.dtype), mv.astype(h.dtype)).reshape(B, T, CA_W)
    return o @ w_co


def moe(h, w_group, b_group, w_expert, b_expert, w_e_gate, w_e_up, w_e_down):
    B, T, Dm = h.shape
    xt = h.reshape(-1, Dm)
    N = xt.shape[0]
    rows = jnp.arange(N)
    gl = (xt @ w_group + b_group).astype(jnp.float32)
    g_idx = jnp.argmax(gl, axis=-1)
    g_w = jax.nn.softmax(gl, axis=-1)[rows, g_idx]
    el = (xt @ w_expert + b_expert).astype(jnp.float32).reshape(N, N_GROUPS, EXPERTS_PER_GROUP)
    el_g = el[rows, g_idx]
    top_v, top_i = lax.top_k(el_g, TOP_K)
    gates = jax.nn.softmax(top_v, axis=-1) * g_w[:, None]
    e_idx = g_idx[:, None] * EXPERTS_PER_GROUP + top_i
    A = N * TOP_K
    e_flat = e_idx.reshape(-1)
    g_flat = gates.reshape(-1)
    tok_flat = jnp.repeat(rows, TOP_K)
    order = jnp.argsort(e_flat)
    e_s, tok_s, g_s = e_flat[order], tok_flat[order], g_flat[order]
    counts = jnp.bincount(e_flat, length=N_EXPERTS)
    off = jnp.cumsum(counts) - counts
    pcounts = (counts + MOE_BLOCK - 1) // MOE_BLOCK * MOE_BLOCK
    pend = jnp.cumsum(pcounts)
    poff = pend - pcounts
    dest = poff[e_s] + jnp.arange(A) - off[e_s]
    P = -(-(A + N_EXPERTS * (MOE_BLOCK - 1)) // MOE_BLOCK) * MOE_BLOCK
    nblk = P // MOE_BLOCK
    row_tok = jnp.zeros((P,), jnp.int32).at[dest].set(tok_s.astype(jnp.int32))
    row_gate = jnp.zeros((P,), jnp.float32).at[dest].set(g_s)
    blk_start = jnp.arange(nblk) * MOE_BLOCK
    blk_e = jnp.minimum(jnp.sum(pend[None, :] <= blk_start[:, None], axis=1), N_EXPERTS - 1)
    xs = xt[row_tok].reshape(nblk, MOE_BLOCK, Dm)

    def expert_block(args):
        xb, e = args
        return (jax.nn.silu(xb @ w_e_gate[e]) * (xb @ w_e_up[e])) @ w_e_down[e]

    ys = lax.map(expert_block, (xs, blk_e)).reshape(P, Dm)
    ys = (ys.astype(jnp.float32) * row_gate[:, None]).astype(xt.dtype)
    out = jnp.zeros_like(xt).at[row_tok].add(ys)
    return out.reshape(B, T, Dm)


def layer(x, prev_conv, h0, past_k, past_v, past_logf, mk, mv, p):
    B, T, _ = x.shape
    h = rmsnorm(x, p['g_mix'])
    xb, gb, q, k, v, fl = split_in(h @ p['w_in'])
    xc, conv_state = causal_conv(xb, prev_conv, p['conv_w'], p['conv_b'])
    hr, h_last = rglru(xc, p['w_rg'], p['b_rg'], p['w_ig'], p['b_ig'], p['lru_lambda'], h0)
    lru_out = hr * jax.nn.gelu(gb)
    logf = jax.nn.log_sigmoid(fl.astype(jnp.float32) + p['b_forget'].astype(jnp.float32))
    fox_out = fox_mixer(q, k, v, logf, past_k, past_v, past_logf).reshape(B, T, FOX_W)
    x = x + jnp.concatenate([lru_out, fox_out.astype(x.dtype)], axis=-1) @ p['w_out']
    x = x + cross_attend(rmsnorm(x, p['g_cross']), mk, mv, p['w_cq'], p['w_co'])
    x = x + moe(rmsnorm(x, p['g_ffn']), p['w_group'], p['b_group'], p['w_expert'], p['b_expert'],
                p['w_e_gate'], p['w_e_up'], p['w_e_down'])
    return x, conv_state, h_last, k, v, logf.astype(x.dtype)


def setup_inputs(seed: int = 0) -> dict:
    key = jax.random.key(seed)
    ks = jax.random.split(key, 40)
    f32 = jnp.float32
    L = DEPTH

    def nrm(i, shape, scale):
        return jax.random.normal(ks[i], shape, f32) * scale

    def gain(i, shape):
        return 1.0 + nrm(i, shape, 0.02)

    u = jax.random.uniform(ks[19], (L, LRU_W), f32, 0.9, 0.999)
    a0 = u ** (1.0 / LRU_C)
    lru_lambda = jnp.log(a0) - jnp.log1p(-a0)
    logf_cache = jax.nn.log_sigmoid(
        jax.random.uniform(ks[6], (L, DEC_BATCH, PAST_LEN, FOX_HEADS), f32, 1.0, 4.0)
        + nrm(7, (L, DEC_BATCH, PAST_LEN, FOX_HEADS), 0.5))
    return {
        'x_prompt': nrm(0, (BATCH, SEQ, D_MODEL), 1.0),
        'x_sample': nrm(1, (DEC_BATCH, DEC_SEQ, D_MODEL), 1.0),
        'cache_conv': nrm(2, (L, DEC_BATCH, CONV_W - 1, LRU_W), 1.0),
        'state_lru': nrm(3, (L, DEC_BATCH, LRU_W), 0.5),
        'cache_fox_k': nrm(4, (L, DEC_BATCH, PAST_LEN, FOX_HEADS, FOX_HEAD_DIM), 1.0),
        'cache_fox_v': nrm(5, (L, DEC_BATCH, PAST_LEN, FOX_HEADS, FOX_HEAD_DIM), 1.0),
        'cache_fox_logf': logf_cache,
        'cache_mem_k': nrm(8, (L, DEC_BATCH, N_MEM, CA_HEADS, CA_HEAD_DIM), 1.0),
        'cache_mem_v': nrm(9, (L, DEC_BATCH, N_MEM, CA_HEADS, CA_HEAD_DIM), 1.0),
        'mem_prompt': nrm(10, (BATCH, N_MEM, D_MODEL), 1.0),
        'g_mix': gain(11, (L, D_MODEL)),
        'w_in': nrm(12, (L, D_MODEL, IN_COLS), D_MODEL ** -0.5),
        'conv_w': nrm(13, (L, CONV_W, LRU_W), CONV_W ** -0.5),
        'conv_b': nrm(14, (L, LRU_W), 0.02),
        'w_rg': nrm(15, (L, LRU_HEADS, LRU_BLK, LRU_BLK), LRU_BLK ** -0.5),
        'b_rg': nrm(16, (L, LRU_HEADS, LRU_BLK), 0.02),
        'w_ig': nrm(17, (L, LRU_HEADS, LRU_BLK, LRU_BLK), LRU_BLK ** -0.5),
        'b_ig': nrm(18, (L, LRU_HEADS, LRU_BLK), 0.02),
        'lru_lambda': lru_lambda,
        'b_forget': jax.random.uniform(ks[20], (L, FOX_HEADS), f32, 1.0, 4.0),
        'w_out': nrm(21, (L, MIX_W, D_MODEL), MIX_W ** -0.5),
        'g_cross': gain(22, (L, D_MODEL)),
        'g_mem': gain(23, (L, D_MODEL)),
        'w_cq': nrm(24, (L, D_MODEL, CA_W), D_MODEL ** -0.5),
        'w_ck': nrm(25, (L, D_MODEL, CA_W), D_MODEL ** -0.5),
        'w_cv': nrm(26, (L, D_MODEL, CA_W), D_MODEL ** -0.5),
        'w_co': nrm(27, (L, CA_W, D_MODEL), CA_W ** -0.5),
        'g_ffn': gain(28, (L, D_MODEL)),
        'w_group': nrm(29, (L, D_MODEL, N_GROUPS), D_MODEL ** -0.5),
        'b_group': nrm(30, (L, N_GROUPS), 0.01),
        'w_expert': nrm(31, (L, D_MODEL, N_EXPERTS), D_MODEL ** -0.5),
        'b_expert': nrm(32, (L, N_EXPERTS), 0.01),
        'w_e_gate': nrm(33, (L, N_EXPERTS, D_MODEL, D_EXPERT), D_MODEL ** -0.5),
        'w_e_up': nrm(34, (L, N_EXPERTS, D_MODEL, D_EXPERT), D_MODEL ** -0.5),
        'w_e_down': nrm(35, (L, N_EXPERTS, D_EXPERT, D_MODEL), D_EXPERT ** -0.5),
        'g_final': gain(36, (D_MODEL,)),
    }


def reference(x_prompt, x_sample, cache_conv, state_lru, cache_fox_k, cache_fox_v, cache_fox_logf,
              cache_mem_k, cache_mem_v, mem_prompt, g_mix, w_in, conv_w, conv_b, w_rg, b_rg, w_ig, b_ig,
              lru_lambda, b_forget, w_out, g_cross, g_mem, w_cq, w_ck, w_cv, w_co, g_ffn, w_group, b_group,
              w_expert, b_expert, w_e_gate, w_e_up, w_e_down, g_final):
    B = x_prompt.shape[0]
    n_mem = mem_prompt.shape[1]
    xp, xs = x_prompt, x_sample
    conv_p, lru_p, fk_p, fv_p, flf_p, mk_p, mv_p = [], [], [], [], [], [], []
    conv_s, lru_s, fk_s, fv_s, flf_s = [], [], [], [], []
    for l in range(DEPTH):
        p = {'g_mix': g_mix[l], 'w_in': w_in[l], 'conv_w': conv_w[l], 'conv_b': conv_b[l],
             'w_rg': w_rg[l], 'b_rg': b_rg[l], 'w_ig': w_ig[l], 'b_ig': b_ig[l],
             'lru_lambda': lru_lambda[l], 'b_forget': b_forget[l], 'w_out': w_out[l],
             'g_cross': g_cross[l], 'w_cq': w_cq[l], 'w_co': w_co[l], 'g_ffn': g_ffn[l],
             'w_group': w_group[l], 'b_group': b_group[l], 'w_expert': w_expert[l], 'b_expert': b_expert[l],
             'w_e_gate': w_e_gate[l], 'w_e_up': w_e_up[l], 'w_e_down': w_e_down[l]}
        mn = rmsnorm(mem_prompt, g_mem[l])
        mk = (mn @ w_ck[l]).reshape(B, n_mem, CA_HEADS, CA_HEAD_DIM)
        mv = (mn @ w_cv[l]).reshape(B, n_mem, CA_HEADS, CA_HEAD_DIM)
        zc = jnp.zeros((B, CONV_W - 1, LRU_W), xp.dtype)
        zh = jnp.zeros((B, LRU_W), xp.dtype)
        xp, c1, h1, k1, v1, f1 = layer(xp, zc, zh, None, None, None, mk, mv, p)
        xs, c2, h2, k2, v2, f2 = layer(xs, cache_conv[l], state_lru[l], cache_fox_k[l], cache_fox_v[l],
                                       cache_fox_logf[l], cache_mem_k[l], cache_mem_v[l], p)
        conv_p.append(c1); lru_p.append(h1); fk_p.append(k1); fv_p.append(v1); flf_p.append(f1)
        mk_p.append(mk); mv_p.append(mv)
        conv_s.append(c2); lru_s.append(h2); fk_s.append(k2); fv_s.append(v2); flf_s.append(f2)
    y_prompt = rmsnorm(xp, g_final)
    y_sample = rmsnorm(xs, g_final)
    return (y_prompt, y_sample,
            jnp.stack(conv_p), jnp.stack(lru_p), jnp.stack(fk_p), jnp.stack(fv_p), jnp.stack(flf_p),
            jnp.stack(mk_p), jnp.stack(mv_p),
            jnp.stack(conv_s), jnp.stack(lru_s), jnp.stack(fk_s), jnp.stack(fv_s), jnp.stack(flf_s))
```

```python
import functools

import jax
import jax.numpy as jnp
from jax import lax
from jax.experimental import pallas as pl
from jax.experimental.pallas import tpu as pltpu

F32 = jnp.float32
BF16 = jnp.bfloat16
I32 = jnp.int32

D_MODEL = 2048
LRU_W = 1024
LRU_HEADS = 8
LRU_BLK = 128
CONV_W = 4
LRU_C = 8.0
FOX_HEADS = 8
FOX_HEAD_DIM = 128
FOX_W = 1024
IN_MAIN = 2 * LRU_W + 3 * FOX_W
LANES = 128
N_MEM = 256
CA_HEADS = 4
CA_HEAD_DIM = 128
CA_W = 512
N_GROUPS = 4
EXPERTS_PER_GROUP = 8
N_EXPERTS = 32
TOP_K = 2
D_EXPERT = 1024
EPS = 1e-6

TM = 256
MOE_TB = 256
VMEM_LIMIT = 56 * 1024 * 1024


def _params(*sem):
    return pltpu.CompilerParams(dimension_semantics=sem, vmem_limit_bytes=VMEM_LIMIT)


def _rmsnorm(x, g):
    return x * lax.rsqrt(jnp.mean(x * x, axis=-1, keepdims=True) + EPS) * g


def _softplus(z):
    return jnp.maximum(z, 0.0) + jnp.log1p(jnp.exp(-jnp.abs(z)))


def _resident(shape):
    nd = len(shape)
    return pl.BlockSpec(shape, lambda *_: (0,) * nd, pipeline_mode=pl.Buffered(1))


def _memkv_kernel(mem_ref, g_ref, wk_ref, wv_ref, mk_ref, mv_ref):
    xn = _rmsnorm(mem_ref[...], g_ref[...]).astype(BF16)
    mk_ref[...] = jnp.dot(xn, wk_ref[...], preferred_element_type=F32)
    mv_ref[...] = jnp.dot(xn, wv_ref[...], preferred_element_type=F32)


def _memkv(mem, g_mem, w_ck, w_cv):
    out = jax.ShapeDtypeStruct((N_MEM, CA_W), F32)
    return pl.pallas_call(
        _memkv_kernel, out_shape=(out, out), name="memkv",
        compiler_params=pltpu.CompilerParams(vmem_limit_bytes=VMEM_LIMIT),
    )(mem, g_mem, w_ck, w_cv)


def _inproj_kernel(xp_ref, xs_ref, g_ref, w_ref, bf_ref,
                   xb_ref, gb_ref, q_ref, kb_ref, vb_ref, lf_ref, kp_ref, vp_ref, ks_ref, vs_ref,
                   *, n_p_tiles):
    is_p = pl.program_id(0) < n_p_tiles
    x = jnp.where(is_p, xp_ref[...], xs_ref[...])
    xn = _rmsnorm(x, g_ref[...]).astype(BF16)

    def proj(c0, width):
        return jnp.dot(xn, w_ref[:, c0:c0 + width], preferred_element_type=F32)

    xb_ref[...] = proj(0, LRU_W)
    gb_ref[...] = proj(LRU_W, LRU_W)
    q_ref[...] = (proj(2 * LRU_W, FOX_W) * (FOX_HEAD_DIM ** -0.5)).astype(BF16)
    k = proj(2 * LRU_W + FOX_W, FOX_W)
    v = proj(2 * LRU_W + 2 * FOX_W, FOX_W)
    kb_ref[...] = k.astype(BF16)
    vb_ref[...] = v.astype(BF16)

    @pl.when(is_p)
    def _():
        kp_ref[...] = k
        vp_ref[...] = v

    @pl.when(jnp.logical_not(is_p))
    def _():
        ks_ref[...] = k
        vs_ref[...] = v

    fl = proj(IN_MAIN, LANES) + bf_ref[...]
    lf_ref[...] = -_softplus(-fl)


def _inproj(xp, xs, g_mix, w_in_pad, bf_pad):
    n_p, n_s = xp.shape[0], xs.shape[0]
    npt, nst = n_p // TM, n_s // TM
    n = n_p + n_s
    p_map = lambda i: (jnp.minimum(i, npt - 1), 0)
    s_map = lambda i: (jnp.maximum(i - npt, 0), 0)
    row = lambda i: (i, 0)
    out_shape = (
        jax.ShapeDtypeStruct((n, LRU_W), F32),
        jax.ShapeDtypeStruct((n, LRU_W), F32),
        jax.ShapeDtypeStruct((n, FOX_W), BF16),
        jax.ShapeDtypeStruct((n, FOX_W), BF16),
        jax.ShapeDtypeStruct((n, FOX_W), BF16),
        jax.ShapeDtypeStruct((n, LANES), F32),
        jax.ShapeDtypeStruct((n_p, FOX_W), F32),
        jax.ShapeDtypeStruct((n_p, FOX_W), F32),
        jax.ShapeDtypeStruct((n_s, FOX_W), F32),
        jax.ShapeDtypeStruct((n_s, FOX_W), F32),
    )
    out_specs = (
        pl.BlockSpec((TM, LRU_W), row), pl.BlockSpec((TM, LRU_W), row),
        pl.BlockSpec((TM, FOX_W), row), pl.BlockSpec((TM, FOX_W), row), pl.BlockSpec((TM, FOX_W), row),
        pl.BlockSpec((TM, LANES), row),
        pl.BlockSpec((TM, FOX_W), p_map), pl.BlockSpec((TM, FOX_W), p_map),
        pl.BlockSpec((TM, FOX_W), s_map), pl.BlockSpec((TM, FOX_W), s_map),
    )
    return pl.pallas_call(
        functools.partial(_inproj_kernel, n_p_tiles=npt),
        grid=(npt + nst,),
        in_specs=[
            pl.BlockSpec((TM, D_MODEL), p_map), pl.BlockSpec((TM, D_MODEL), s_map),
            _resident((1, D_MODEL)), _resident(w_in_pad.shape), _resident((1, LANES)),
        ],
        out_specs=out_specs, out_shape=out_shape, name="inproj",
        compiler_params=_params("arbitrary"),
    )(xp, xs, g_mix, w_in_pad, bf_pad)


def _cumsum_kernel(x_ref, o_ref, carry_ref, *, tw):
    @pl.when(pl.program_id(0) == 0)
    def _():
        carry_ref[...] = jnp.zeros_like(carry_ref)

    x = x_ref[...]
    hi = x.astype(BF16)
    r1 = x - hi.astype(F32)
    mid = r1.astype(BF16)
    lo = (r1 - mid.astype(F32)).astype(BF16)
    src = lax.broadcasted_iota(I32, (tw, tw), 0)
    dst = lax.broadcasted_iota(I32, (tw, tw), 1)
    tri = (src <= dst).astype(BF16)
    c = (jnp.dot(hi, tri, preferred_element_type=F32)
         + jnp.dot(mid, tri, preferred_element_type=F32)
         + jnp.dot(lo, tri, preferred_element_type=F32))
    out = c + carry_ref[...]
    o_ref[...] = out
    carry_ref[...] = out[:, tw - 1:tw]


def _cumsum_lanes(x, tw):
    r, length = x.shape
    return pl.pallas_call(
        functools.partial(_cumsum_kernel, tw=tw),
        grid=(length // tw,),
        in_specs=[pl.BlockSpec((r, tw), lambda i: (0, i))],
        out_specs=pl.BlockSpec((r, tw), lambda i: (0, i)),
        out_shape=jax.ShapeDtypeStruct((r, length), F32),
        scratch_shapes=[pltpu.VMEM((r, 1), F32)], name="cumsum",
        compiler_params=_params("arbitrary"),
    )(x)


def _lru_kernel(xb_ref, gb_ref, cache_ref, h0_ref, cw_ref, cb_ref, wrg_ref, brg_ref, wig_ref, big_ref,
                lam_ref, out_ref, hl_ref, ext_ref, hcar_ref, *, nb, rows):
    t = pl.program_id(1)
    m = nb * rows
    c = LRU_W

    @pl.when(t == 0)
    def _():
        ext_ref[:, 0:8, :] = cache_ref[...]
        hcar_ref[...] = h0_ref[...]

    @pl.when(t > 0)
    def _():
        ext_ref[:, 0:8, :] = ext_ref[:, rows:rows + 8, :]

    x = xb_ref[...]
    ext_ref[:, 8:, :] = x.reshape(nb, rows, c)
    cw = cw_ref[...]
    xc = cb_ref[...] + x * cw[CONV_W - 1:CONV_W]
    for j in range(1, CONV_W):
        xj = ext_ref[:, 8 - j:8 - j + rows, :].reshape(m, c)
        xc = xc + xj * cw[CONV_W - 1 - j:CONV_W - j]

    xcb = xc.astype(BF16)

    def gate(w_ref, b_ref):
        parts = [jnp.dot(xcb[:, h * LRU_BLK:(h + 1) * LRU_BLK], w_ref[h], preferred_element_type=F32)
                 for h in range(LRU_HEADS)]
        return jax.nn.sigmoid(jnp.concatenate(parts, axis=-1) + b_ref[...])

    r = gate(wrg_ref, brg_ref)
    ig = gate(wig_ref, big_ref)
    log_a = (-LRU_C) * r * _softplus(-lam_ref[...])
    a = jnp.exp(log_a)
    b = jnp.sqrt(1.0 - a * a) * (ig * xc)

    pos = lax.broadcasted_iota(I32, (m, 1), 0) & (rows - 1)
    s = 1
    while s < rows:
        a_prev = pltpu.roll(a, s, 0)
        b_prev = pltpu.roll(b, s, 0)
        take = pos >= s
        b = jnp.where(take, a * b_prev + b, b)
        a = jnp.where(take, a * a_prev, a)
        s *= 2

    h = a.reshape(nb, rows, c) * hcar_ref[...] + b.reshape(nb, rows, c)
    h_last = h[:, rows - 1:rows, :]
    hcar_ref[...] = h_last
    hl_ref[...] = h_last
    gb = gb_ref[...]
    gelu = 0.5 * gb * (1.0 + jnp.tanh(0.7978845608028654 * (gb + 0.044715 * (gb * gb * gb))))
    out_ref[...] = (h.reshape(m, c) * gelu).astype(BF16)


def _lru(xb, gb, cache8, h0, conv_w, conv_b, w_rg, b_rg, w_ig, b_ig, lam, *, n_batch, t_len, nb, rows):
    nt = t_len // rows
    m = nb * rows
    blk = lambda b, t: (b * nt + t, 0)
    per_b = lambda b, t: (b, 0, 0)
    return pl.pallas_call(
        functools.partial(_lru_kernel, nb=nb, rows=rows),
        grid=(n_batch // nb, nt),
        in_specs=[
            pl.BlockSpec((m, LRU_W), blk), pl.BlockSpec((m, LRU_W), blk),
            pl.BlockSpec((nb, 8, LRU_W), per_b), pl.BlockSpec((nb, 1, LRU_W), per_b),
            _resident((CONV_W, LRU_W)), _resident((1, LRU_W)),
            _resident((LRU_HEADS, LRU_BLK, LRU_BLK)), _resident((1, LRU_W)),
            _resident((LRU_HEADS, LRU_BLK, LRU_BLK)), _resident((1, LRU_W)),
            _resident((1, LRU_W)),
        ],
        out_specs=(pl.BlockSpec((m, LRU_W), blk), pl.BlockSpec((nb, 1, LRU_W), per_b)),
        out_shape=(jax.ShapeDtypeStruct((n_batch * t_len, LRU_W), BF16),
                   jax.ShapeDtypeStruct((n_batch, 1, LRU_W), F32)),
        scratch_shapes=[pltpu.VMEM((nb, 8 + rows, LRU_W), F32), pltpu.VMEM((nb, 1, LRU_W), F32)],
        name="lru", compiler_params=_params("arbitrary", "arbitrary"),
    )(xb, gb, cache8, h0, conv_w, conv_b, w_rg, b_rg, w_ig, b_ig, lam)


def _fox_p_kernel(q_ref, k_ref, v_ref, f_ref, o_ref, *, tq, tk):
    qi = pl.program_id(1)
    q = q_ref[...]
    q0 = qi * tq

    def block(kj, carry, masked):
        m_i, l_i, acc = carry
        k0 = pl.multiple_of(kj * tk, tk)
        s = lax.dot_general(q, k_ref[pl.ds(k0, tk), :], (((1,), (1,)), ((), ())),
                            preferred_element_type=F32)
        s = s - f_ref[0, :, pl.ds(k0, tk)]
        if masked:
            qpos = q0 + lax.broadcasted_iota(I32, (tq, tk), 0)
            kpos = k0 + lax.broadcasted_iota(I32, (tq, tk), 1)
            s = jnp.where(kpos <= qpos, s, -jnp.inf)
        m_new = jnp.maximum(m_i, jnp.max(s, axis=-1, keepdims=True))
        alpha = jnp.exp(m_i - m_new)
        p = jnp.exp(s - m_new)
        l_new = alpha * l_i + jnp.sum(p, axis=-1, keepdims=True)
        acc = alpha * acc + jnp.dot(p.astype(BF16), v_ref[pl.ds(k0, tk), :], preferred_element_type=F32)
        return m_new, l_new, acc

    init = (jnp.full((tq, 1), -jnp.inf, F32), jnp.zeros((tq, 1), F32), jnp.zeros((tq, FOX_HEAD_DIM), F32))
    n_full = q0 // tk
    carry = lax.fori_loop(0, n_full, lambda kj, cr: block(kj, cr, False), init)
    _, l_i, acc = block(n_full, carry, True)
    o_ref[...] = (acc / l_i).astype(BF16)


def _fox_p(q, kb, vb, f_rows, *, n_p, tq, tk):
    assert tk % tq == 0
    d = FOX_HEAD_DIM
    return pl.pallas_call(
        functools.partial(_fox_p_kernel, tq=tq, tk=tk),
        grid=(FOX_HEADS, n_p // tq),
        in_specs=[
            pl.BlockSpec((tq, d), lambda h, i: (i, h)),
            pl.BlockSpec((n_p, d), lambda h, i: (0, h)),
            pl.BlockSpec((n_p, d), lambda h, i: (0, h)),
            pl.BlockSpec((1, 1, n_p), lambda h, i: (h, 0, 0)),
        ],
        out_specs=pl.BlockSpec((tq, d), lambda h, i: (i, h)),
        out_shape=jax.ShapeDtypeStruct((n_p, FOX_W), BF16),
        name="fox_p", compiler_params=_params("arbitrary", "arbitrary"),
    )(q, kb, vb, f_rows)


def _fox_s_kernel(q_ref, kn_ref, vn_ref, ck_ref, cv_ref, f_ref, o_ref, *, t_new, past):
    d = FOX_HEAD_DIM
    nt = (((1,), (1,)), ((), ()))
    qpos = lax.broadcasted_iota(I32, (t_new, t_new), 0)
    kpos = lax.broadcasted_iota(I32, (t_new, t_new), 1)
    outs = []
    for h in range(FOX_HEADS):
        sl = slice(h * d, (h + 1) * d)
        q = q_ref[:, sl]
        s_past = lax.dot_general(q, ck_ref[0, :, sl].astype(BF16), nt, preferred_element_type=F32)
        s_past = s_past - f_ref[0, h:h + 1, 0:past]
        s_new = lax.dot_general(q, kn_ref[:, sl], nt, preferred_element_type=F32)
        s_new = jnp.where(kpos <= qpos, s_new - f_ref[0, h:h + 1, past:past + t_new], -jnp.inf)
        m_i = jnp.maximum(jnp.max(s_past, axis=-1, keepdims=True), jnp.max(s_new, axis=-1, keepdims=True))
        p_past = jnp.exp(s_past - m_i)
        p_new = jnp.exp(s_new - m_i)
        l_i = jnp.sum(p_past, axis=-1, keepdims=True) + jnp.sum(p_new, axis=-1, keepdims=True)
        acc = (jnp.dot(p_past.astype(BF16), cv_ref[0, :, sl].astype(BF16), preferred_element_type=F32)
               + jnp.dot(p_new.astype(BF16), vn_ref[:, sl], preferred_element_type=F32))
        outs.append(acc / l_i)
    o_ref[...] = jnp.concatenate(outs, axis=-1).astype(BF16)


def _fox_s(q, kb, vb, cache_k, cache_v, f_rows, *, row0, n_batch, t_new, past):
    blk0 = row0 // t_new
    tok = lambda b: (blk0 + b, 0)
    lpad = f_rows.shape[-1]
    return pl.pallas_call(
        functools.partial(_fox_s_kernel, t_new=t_new, past=past),
        grid=(n_batch,),
        in_specs=[
            pl.BlockSpec((t_new, FOX_W), tok), pl.BlockSpec((t_new, FOX_W), tok), pl.BlockSpec((t_new, FOX_W), tok),
            pl.BlockSpec((1, past, FOX_W), lambda b: (b, 0, 0)),
            pl.BlockSpec((1, past, FOX_W), lambda b: (b, 0, 0)),
            pl.BlockSpec((1, FOX_HEADS, lpad), lambda b: (b, 0, 0)),
        ],
        out_specs=pl.BlockSpec((t_new, FOX_W), lambda b: (b, 0)),
        out_shape=jax.ShapeDtypeStruct((n_batch * t_new, FOX_W), BF16),
        name="fox_s", compiler_params=_params("arbitrary"),
    )(q, kb, vb, cache_k, cache_v, f_rows)


def _cross_heads(qc, mk, mv):
    nt = (((1,), (1,)), ((), ()))
    outs = []
    for h in range(CA_HEADS):
        sl = slice(h * CA_HEAD_DIM, (h + 1) * CA_HEAD_DIM)
        s = lax.dot_general(qc[:, sl], mk[:, sl], nt, preferred_element_type=F32)
        p = jnp.exp(s - jnp.max(s, axis=-1, keepdims=True))
        p = p / jnp.sum(p, axis=-1, keepdims=True)
        outs.append(jnp.dot(p.astype(BF16), mv[:, sl], preferred_element_type=F32))
    return jnp.concatenate(outs, axis=-1)


def _mix_kernel(xp_ref, xs_ref, lp_ref, ls_ref, fp_ref, fs_ref, wo_ref, gc_ref, wcq_ref,
                mkp_ref, mvp_ref, cmk_ref, cmv_ref, wco_ref, gf_ref, wr_ref, br_ref,
                x2_ref, h3_ref, ridx_ref, rgate_ref, ca_ref, *, n_p_tiles, s_batches, s_rows):
    is_p = pl.program_id(0) < n_p_tiles
    x = jnp.where(is_p, xp_ref[...], xs_ref[...])
    lru = jnp.where(is_p, lp_ref[...], ls_ref[...])
    fox = jnp.where(is_p, fp_ref[...], fs_ref[...])
    x1 = (x + jnp.dot(lru, wo_ref[0:LRU_W, :], preferred_element_type=F32)
          + jnp.dot(fox, wo_ref[LRU_W:, :], preferred_element_type=F32))

    hn = _rmsnorm(x1, gc_ref[...]).astype(BF16)
    qc = (jnp.dot(hn, wcq_ref[...], preferred_element_type=F32) * (CA_HEAD_DIM ** -0.5)).astype(BF16)

    @pl.when(is_p)
    def _():
        ca_ref[...] = _cross_heads(qc, mkp_ref[...].astype(BF16), mvp_ref[...].astype(BF16)).astype(BF16)

    @pl.when(jnp.logical_not(is_p))
    def _():
        for b in range(s_batches):
            rs = slice(b * s_rows, (b + 1) * s_rows)
            ca_ref[rs, :] = _cross_heads(qc[rs, :], cmk_ref[b].astype(BF16), cmv_ref[b].astype(BF16)).astype(BF16)

    x2 = x1 + jnp.dot(ca_ref[...], wco_ref[...], preferred_element_type=F32)
    x2_ref[...] = x2
    h3 = _rmsnorm(x2, gf_ref[...])
    h3_ref[...] = h3

    logit = jnp.dot(h3.astype(BF16), wr_ref[...], preferred_element_type=F32) + br_ref[...]
    lane = lax.broadcasted_iota(I32, logit.shape, 1)
    neg = -jnp.inf

    def first_max(vals):
        top = jnp.max(vals, axis=-1, keepdims=True)
        idx = jnp.min(jnp.where(vals == top, lane, LANES), axis=-1, keepdims=True)
        return top, idx

    is_group = lane < N_GROUPS
    g_top, g_idx = first_max(jnp.where(is_group, logit, neg))
    g_w = 1.0 / jnp.sum(jnp.where(is_group, jnp.exp(logit - g_top), 0.0), axis=-1, keepdims=True)
    lo = N_GROUPS + EXPERTS_PER_GROUP * g_idx
    el = jnp.where((lane >= lo) & (lane < lo + EXPERTS_PER_GROUP), logit, neg)
    v1, i1 = first_max(el)
    v2, i2 = first_max(jnp.where(lane == i1, neg, el))
    t = jnp.exp(v2 - v1)
    gate1 = (1.0 / (1.0 + t)) * g_w
    gate2 = (t / (1.0 + t)) * g_w
    ridx_ref[...] = jnp.where(lane == 0, i1 - N_GROUPS, jnp.where(lane == 1, i2 - N_GROUPS, 0))
    rgate_ref[...] = jnp.where(lane == 0, gate1, jnp.where(lane == 1, gate2, 0.0))


def _mix(xp, xs, lru_p, lru_s, fox_p, fox_s, w_out, g_cross, w_cq, mk_p, mv_p, cmk, cmv, w_co, g_ffn,
         w_router, b_router, *, s_rows):
    n_p, n_s = xp.shape[0], xs.shape[0]
    npt, nst = n_p // TM, n_s // TM
    n = n_p + n_s
    s_batches = TM // s_rows
    p_map = lambda i: (jnp.minimum(i, npt - 1), 0)
    s_map = lambda i: (jnp.maximum(i - npt, 0), 0)
    s_map3 = lambda i: (jnp.maximum(i - npt, 0), 0, 0)
    row = lambda i: (i, 0)
    return pl.pallas_call(
        functools.partial(_mix_kernel, n_p_tiles=npt, s_batches=s_batches, s_rows=s_rows),
        grid=(npt + nst,),
        in_specs=[
            pl.BlockSpec((TM, D_MODEL), p_map), pl.BlockSpec((TM, D_MODEL), s_map),
            pl.BlockSpec((TM, LRU_W), p_map), pl.BlockSpec((TM, LRU_W), s_map),
            pl.BlockSpec((TM, FOX_W), p_map), pl.BlockSpec((TM, FOX_W), s_map),
            _resident((D_MODEL, D_MODEL)), _resident((1, D_MODEL)), _resident((D_MODEL, CA_W)),
            _resident((N_MEM, CA_W)), _resident((N_MEM, CA_W)),
            pl.BlockSpec((s_batches, N_MEM, CA_W), s_map3), pl.BlockSpec((s_batches, N_MEM, CA_W), s_map3),
            _resident((CA_W, D_MODEL)), _resident((1, D_MODEL)),
            _resident((D_MODEL, LANES)), _resident((1, LANES)),
        ],
        out_specs=(pl.BlockSpec((TM, D_MODEL), row), pl.BlockSpec((TM, D_MODEL), row),
                   pl.BlockSpec((TM, LANES), row), pl.BlockSpec((TM, LANES), row)),
        out_shape=(jax.ShapeDtypeStruct((n, D_MODEL), F32), jax.ShapeDtypeStruct((n, D_MODEL), F32),
                   jax.ShapeDtypeStruct((n, LANES), I32), jax.ShapeDtypeStruct((n, LANES), F32)),
        scratch_shapes=[pltpu.VMEM((TM, CA_W), BF16)],
        name="mix", compiler_params=_params("arbitrary"),
    )(xp, xs, lru_p, lru_s, fox_p, fox_s, w_out, g_cross, w_cq, mk_p, mv_p, cmk, cmv, w_co, g_ffn,
      w_router, b_router)


def _experts_kernel(blk_e_ref, blk_cnt_ref, tok_ref, dst_ref, gate_ref, h3_hbm, wg_ref, wu_ref, wd_ref,
                    y_hbm, xbuf, ybuf, gsem, ssem):
    i = pl.program_id(0)
    tb = xbuf.shape[0]

    @pl.when(i == 0)
    def _():
        ybuf[...] = jnp.zeros_like(ybuf)
        spare = pltpu.make_async_copy(ybuf, y_hbm.at[pl.ds(y_hbm.shape[0] - tb, tb)], ssem)
        spare.start()
        spare.wait()

    @pl.when(blk_cnt_ref[i] > 0)
    def _():
        def gather(r, carry):
            pltpu.make_async_copy(h3_hbm.at[pl.ds(tok_ref[0, 0, r], 1)], xbuf.at[pl.ds(r, 1)], gsem).start()
            return carry

        lax.fori_loop(0, tb, gather, 0)
        pltpu.make_async_copy(h3_hbm.at[pl.ds(0, tb)], xbuf, gsem).wait()

        x = xbuf[...].astype(BF16)
        g = jnp.dot(x, wg_ref[0], preferred_element_type=F32)
        u = jnp.dot(x, wu_ref[0], preferred_element_type=F32)
        hmid = (g * jax.nn.sigmoid(g) * u).astype(BF16)
        y = jnp.dot(hmid, wd_ref[0], preferred_element_type=F32)
        ybuf[...] = y * gate_ref[...]

        def scatter(r, carry):
            pltpu.make_async_copy(ybuf.at[pl.ds(r, 1)], y_hbm.at[pl.ds(dst_ref[0, 0, r], 1)], ssem).start()
            return carry

        lax.fori_loop(0, tb, scatter, 0)
        pltpu.make_async_copy(ybuf, y_hbm.at[pl.ds(0, tb)], ssem).wait()


def _experts(blk_e, blk_cnt, row_tok, row_dst, row_gate, h3, w_gate, w_up, w_down, *, n_out_rows):
    nblk = blk_e.shape[0]
    tb = MOE_TB
    smem_rows = pl.BlockSpec((1, 1, tb), lambda i, *_: (i, 0, 0), memory_space=pltpu.SMEM)
    w_in_spec = pl.BlockSpec((1, D_MODEL, D_EXPERT), lambda i, be, bc: (be[i], 0, 0))
    w_dn_spec = pl.BlockSpec((1, D_EXPERT, D_MODEL), lambda i, be, bc: (be[i], 0, 0))
    grid_spec = pltpu.PrefetchScalarGridSpec(
        num_scalar_prefetch=2, grid=(nblk,),
        in_specs=[
            smem_rows, smem_rows,
            pl.BlockSpec((tb, 1), lambda i, *_: (i, 0)),
            pl.BlockSpec(memory_space=pl.ANY),
            w_in_spec, w_in_spec, w_dn_spec,
        ],
        out_specs=pl.BlockSpec(memory_space=pl.ANY),
        scratch_shapes=[pltpu.VMEM((tb, D_MODEL), F32), pltpu.VMEM((tb, D_MODEL), F32),
                        pltpu.SemaphoreType.DMA, pltpu.SemaphoreType.DMA],
    )
    return pl.pallas_call(
        _experts_kernel, grid_spec=grid_spec,
        out_shape=jax.ShapeDtypeStruct((n_out_rows, D_MODEL), F32),
        name="experts", compiler_params=_params("arbitrary"),
    )(blk_e, blk_cnt, row_tok, row_dst, row_gate, h3, w_gate, w_up, w_down)


def _final_kernel(x2_ref, y0_ref, y1_ref, g_ref, yp_ref, ys_ref, *, n_p_tiles):
    is_p = pl.program_id(0) < n_p_tiles
    out = _rmsnorm(x2_ref[...] + (y0_ref[...] + y1_ref[...]), g_ref[...])

    @pl.when(is_p)
    def _():
        yp_ref[...] = out

    @pl.when(jnp.logical_not(is_p))
    def _():
        ys_ref[...] = out


def _final(x2, y2, g_final, *, n_p, n_s):
    npt, nst = n_p // TM, n_s // TM
    return pl.pallas_call(
        functools.partial(_final_kernel, n_p_tiles=npt),
        grid=(npt + nst,),
        in_specs=[pl.BlockSpec((TM, D_MODEL), lambda i: (i, 0)),
                  pl.BlockSpec((TM, D_MODEL), lambda i: (i, 0)),
                  pl.BlockSpec((TM, D_MODEL), lambda i: (npt + nst + i, 0)),
                  _resident((1, D_MODEL))],
        out_specs=(pl.BlockSpec((TM, D_MODEL), lambda i: (jnp.minimum(i, npt - 1), 0)),
                   pl.BlockSpec((TM, D_MODEL), lambda i: (jnp.maximum(i - npt, 0), 0))),
        out_shape=(jax.ShapeDtypeStruct((n_p, D_MODEL), F32), jax.ShapeDtypeStruct((n_s, D_MODEL), F32)),
        name="final", compiler_params=_params("arbitrary"),
    )(x2, y2, y2, g_final)


def _dispatch_plan(ridx, rgate, n):
    tb = MOE_TB
    a = n * TOP_K
    nblk = -(-(a + N_EXPERTS * (tb - 1)) // tb)
    e_flat = ridx[:, :TOP_K].reshape(-1)
    g_flat = rgate[:, :TOP_K].reshape(-1)
    onehot = (e_flat[:, None] == jnp.arange(N_EXPERTS, dtype=I32)[None, :]).astype(I32)
    csum = jnp.cumsum(onehot, axis=0)
    counts = csum[-1]
    rank = jnp.sum((csum - onehot) * onehot, axis=1)
    pcounts = (counts + tb - 1) // tb * tb
    pend = jnp.cumsum(pcounts)
    poff = pend - pcounts
    dest = poff[e_flat] + rank
    tok_flat = jnp.arange(a, dtype=I32) // TOP_K
    k_flat = jnp.arange(a, dtype=I32) % TOP_K
    p = nblk * tb
    row_tok = jnp.zeros((p,), I32).at[dest].set(tok_flat)
    pad_dst = a + jnp.arange(p, dtype=I32) % tb
    row_dst = pad_dst.at[dest].set(k_flat * n + tok_flat)
    row_gate = jnp.zeros((p,), F32).at[dest].set(g_flat)
    blk_start = jnp.arange(nblk, dtype=I32) * tb
    blk_e = jnp.minimum(jnp.sum((pend[None, :] <= blk_start[:, None]).astype(I32), axis=1), N_EXPERTS - 1)
    blk_cnt = jnp.clip(poff[blk_e] + counts[blk_e] - blk_start, 0, tb).astype(I32)
    return (blk_e.astype(I32), blk_cnt, row_tok.reshape(nblk, 1, tb), row_dst.reshape(nblk, 1, tb),
            row_gate.reshape(p, 1))


def kernel(x_prompt, x_sample, cache_conv, state_lru, cache_fox_k, cache_fox_v, cache_fox_logf, cache_mem_k, cache_mem_v, mem_prompt, g_mix, w_in, conv_w, conv_b, w_rg, b_rg, w_ig, b_ig, lru_lambda, b_forget, w_out, g_cross, g_mem, w_cq, w_ck, w_cv, w_co, g_ffn, w_group, b_group, w_expert, b_expert, w_e_gate, w_e_up, w_e_down, g_final):
    depth = g_mix.shape[0]
    assert depth == 1 and x_prompt.shape[0] == 1
    n_p = x_prompt.shape[1]
    sb, st = x_sample.shape[0], x_sample.shape[1]
    n_s = sb * st
    n = n_p + n_s
    past = cache_fox_k.shape[2]
    l = 0

    xp = x_prompt.reshape(n_p, D_MODEL)
    xs = x_sample.reshape(n_s, D_MODEL)

    w_in_pad = jnp.pad(w_in[l].astype(BF16), ((0, 0), (0, LANES - FOX_HEADS)))
    bf_pad = jnp.pad(b_forget[l], (0, LANES - FOX_HEADS)).reshape(1, LANES)
    w_router = jnp.pad(jnp.concatenate([w_group[l], w_expert[l]], axis=1).astype(BF16),
                       ((0, 0), (0, LANES - N_GROUPS - N_EXPERTS)))
    b_router = jnp.pad(jnp.concatenate([b_group[l], b_expert[l]]), (0, LANES - N_GROUPS - N_EXPERTS)).reshape(1, LANES)
    row2 = lambda v: v.reshape(1, -1)

    mk_p, mv_p = _memkv(mem_prompt[0], row2(g_mem[l]), w_ck[l].astype(BF16), w_cv[l].astype(BF16))

    xb, gb, q, kb, vb, lf, k_p, v_p, k_s, v_s = _inproj(xp, xs, row2(g_mix[l]), w_in_pad, bf_pad)
    logf = lf[:, :FOX_HEADS]
    logf_p = logf[:n_p]
    logf_s = logf[n_p:].reshape(sb, st, FOX_HEADS)

    f_p = _cumsum_lanes(logf_p.T, 512).reshape(FOX_HEADS, 1, n_p)
    lf_all_s = jnp.concatenate([jnp.swapaxes(cache_fox_logf[l], 1, 2), jnp.swapaxes(logf_s, 1, 2)], axis=2)
    lpad = -(-(past + st) // LANES) * LANES
    lf_all_s = jnp.pad(lf_all_s, ((0, 0), (0, 0), (0, lpad - past - st))).reshape(sb * FOX_HEADS, lpad)
    f_s = _cumsum_lanes(lf_all_s, lpad).reshape(sb, FOX_HEADS, lpad)

    lru_w = (conv_w[l], row2(conv_b[l]), w_rg[l].astype(BF16), row2(b_rg[l]), w_ig[l].astype(BF16),
             row2(b_ig[l]), row2(lru_lambda[l]))
    lru_p, hl_p = _lru(xb, gb, jnp.zeros((1, 8, LRU_W), F32), jnp.zeros((1, 1, LRU_W), F32), *lru_w,
                       n_batch=1, t_len=n_p, nb=1, rows=TM)
    cache8 = jnp.pad(cache_conv[l], ((0, 0), (8 - (CONV_W - 1), 0), (0, 0)))
    lru_s, hl_s = _lru(xb[n_p:], gb[n_p:], cache8, state_lru[l].reshape(sb, 1, LRU_W), *lru_w,
                       n_batch=sb, t_len=st, nb=TM // st, rows=st)

    fox_p = _fox_p(q, kb, vb, f_p, n_p=n_p, tq=256, tk=512)
    fox_s = _fox_s(q, kb, vb, cache_fox_k[l].reshape(sb, past, FOX_W), cache_fox_v[l].reshape(sb, past, FOX_W),
                   f_s, row0=n_p, n_batch=sb, t_new=st, past=past)

    x2, h3, ridx, rgate = _mix(
        xp, xs, lru_p, lru_s, fox_p, fox_s, w_out[l].astype(BF16), row2(g_cross[l]), w_cq[l].astype(BF16),
        mk_p, mv_p, cache_mem_k[l].reshape(sb, N_MEM, CA_W), cache_mem_v[l].reshape(sb, N_MEM, CA_W),
        w_co[l].astype(BF16), row2(g_ffn[l]), w_router, b_router, s_rows=st)

    blk_e, blk_cnt, row_tok, row_dst, row_gate = _dispatch_plan(ridx, rgate, n)
    y2 = _experts(blk_e, blk_cnt, row_tok, row_dst, row_gate, h3, w_e_gate[l].astype(BF16),
                  w_e_up[l].astype(BF16), w_e_down[l].astype(BF16), n_out_rows=TOP_K * n + MOE_TB)
    y_p, y_s = _final(x2, y2, row2(g_final), n_p=n_p, n_s=n_s)

    hd = (FOX_HEADS, FOX_HEAD_DIM)
    xb_s = xb[n_p:].reshape(sb, st, LRU_W)
    return (
        y_p.reshape(1, n_p, D_MODEL), y_s.reshape(sb, st, D_MODEL),
        xb[n_p - (CONV_W - 1):n_p].reshape(1, 1, CONV_W - 1, LRU_W), hl_p.reshape(1, 1, LRU_W),
        k_p.reshape(1, 1, n_p, *hd), v_p.reshape(1, 1, n_p, *hd), logf_p.reshape(1, 1, n_p, FOX_HEADS),
        mk_p.reshape(1, 1, N_MEM, CA_HEADS, CA_HEAD_DIM), mv_p.reshape(1, 1, N_MEM, CA_HEADS, CA_HEAD_DIM),
        xb_s[:, st - (CONV_W - 1):].reshape(1, sb, CONV_W - 1, LRU_W), hl_s.reshape(1, sb, LRU_W),
        k_s.reshape(1, sb, st, *hd), v_s.reshape(1, sb, st, *hd), logf_s.reshape(1, sb, st, FOX_HEADS),
    )
```
